```python
import jax, jax.numpy as jnp
from jax import lax
import numpy as np


D_MODEL = 1024
BATCH = 16
SEQ = 2048
DEPTH = 4

GRID_W = 64
HEAD_DIM = 64
N_Q_HEADS = 8
N_KV_HEADS = 2
Q_PER_KV = N_Q_HEADS // N_KV_HEADS
ATTN_WIDTH = N_Q_HEADS * HEAD_DIM
KV_WIDTH = N_KV_HEADS * HEAD_DIM
ROPE_THETA = 10000.0
Q_BLOCK = 128
CONV_WIDTH = D_MODEL // 2
POOL_WINDOWS = (2, 4, 8, 16)
N_POOL_GROUPS = 4
POOL_WIDTH = D_MODEL // 2
POOL_GROUP = POOL_WIDTH // N_POOL_GROUPS
SGU_WIDTH = D_MODEL // 2
N_SGU_GROUPS = 4
SGU_GROUP = SGU_WIDTH // N_SGU_GROUPS
SGU_CHUNK = 128
D_FF = 2816
EPS = 1e-6
EVEN_IN = 3 * CONV_WIDTH + ATTN_WIDTH + 2 * KV_WIDTH
EVEN_SPLITS = (CONV_WIDTH, 2 * CONV_WIDTH, 3 * CONV_WIDTH,
               3 * CONV_WIDTH + ATTN_WIDTH, 3 * CONV_WIDTH + ATTN_WIDTH + KV_WIDTH)
EVEN_MIX = CONV_WIDTH + ATTN_WIDTH
ODD_IN = POOL_WIDTH + 2 * SGU_WIDTH
ODD_SPLITS = (POOL_WIDTH, POOL_WIDTH + SGU_WIDTH)
ODD_MIX = POOL_WIDTH + SGU_WIDTH

kernel_name = 'hybrid_conv_gqa_pool_sgu_macaron_encoder'


def rms_norm(x, g):
    xf = x.astype(jnp.float32)
    y = xf * lax.rsqrt(jnp.mean(xf * xf, axis=-1, keepdims=True) + EPS)
    return (y * g.astype(jnp.float32)).astype(x.dtype)


def swiglu(x, w_in, w_out):
    g, u = jnp.split(x @ w_in, 2, axis=-1)
    return (jax.nn.silu(g) * u) @ w_out


def axial_rope_tables(seq):
    rows = seq // GRID_W
    r_idx, c_idx = jnp.meshgrid(jnp.arange(rows), jnp.arange(GRID_W), indexing='ij')
    r_idx = r_idx.reshape(-1).astype(jnp.float32)
    c_idx = c_idx.reshape(-1).astype(jnp.float32)
    n_freq = HEAD_DIM // 4
    inv = ROPE_THETA ** (-jnp.arange(n_freq, dtype=jnp.float32) / n_freq)
    ang = jnp.concatenate([r_idx[:, None] * inv, c_idx[:, None] * inv], axis=-1)
    return jnp.cos(ang), jnp.sin(ang)


def apply_rope(x, cos, sin):
    b, s, h, d = x.shape
    xf = x.astype(jnp.float32).reshape(b, s, h, d // 2, 2)
    x0, x1 = xf[..., 0], xf[..., 1]
    c = cos[None, :, None, :]
    sn = sin[None, :, None, :]
    out = jnp.stack([x0 * c - x1 * sn, x0 * sn + x1 * c], axis=-1)
    return out.reshape(b, s, h, d).astype(x.dtype)


def blocked_gqa(q, k, v):
    b, s, _, _ = q.shape
    nblk = s // Q_BLOCK
    qb = q.reshape(b, nblk, Q_BLOCK, N_KV_HEADS, Q_PER_KV, HEAD_DIM).transpose(1, 0, 2, 3, 4, 5)
    scale = HEAD_DIM ** -0.5

    def one_block(qi):
        sc = jnp.einsum('bqkgd,bskd->bkgqs', qi, k).astype(jnp.float32) * scale
        p = jax.nn.softmax(sc, axis=-1).astype(v.dtype)
        return jnp.einsum('bkgqs,bskd->bqkgd', p, v)

    o = lax.map(one_block, qb)
    return o.transpose(1, 0, 2, 3, 4, 5).reshape(b, s, ATTN_WIDTH)


def short_conv(h, w):
    hp = jnp.pad(h, ((0, 0), (1, 1), (0, 0)))
    return hp[:, :-2] * w[0] + hp[:, 1:-1] * w[1] + hp[:, 2:] * w[2]


def even_mixer(h, w_in, conv_w, q_g, k_g, w_out, cos, sin):
    b, s, _ = h.shape
    gate_b, gate_c, hc, q, k, v = jnp.split(h @ w_in, list(EVEN_SPLITS), axis=-1)
    a_out = gate_b * short_conv(gate_c * hc, conv_w)
    q = apply_rope(rms_norm(q.reshape(b, s, N_Q_HEADS, HEAD_DIM), q_g), cos, sin)
    k = apply_rope(rms_norm(k.reshape(b, s, N_KV_HEADS, HEAD_DIM), k_g), cos, sin)
    v = v.reshape(b, s, N_KV_HEADS, HEAD_DIM)
    b_out = blocked_gqa(q, k, v)
    return jnp.concatenate([a_out, b_out], axis=-1) @ w_out


def multiscale_pool(p):
    b, s, _ = p.shape
    pf = p.astype(jnp.float32)
    cs = jnp.concatenate([jnp.zeros((b, 1, POOL_WIDTH), jnp.float32), jnp.cumsum(pf, axis=1)], axis=1)
    t = jnp.arange(s)
    outs = []
    for gi, w in enumerate(POOL_WINDOWS):
        r = w // 2
        lo = jnp.maximum(t - r, 0)
        hi = jnp.minimum(t + r, s - 1)
        sl = slice(gi * POOL_GROUP, (gi + 1) * POOL_GROUP)
        csg = cs[:, :, sl]
        win = csg[:, hi + 1] - csg[:, lo]
        cnt = (hi - lo + 1).astype(jnp.float32)[None, :, None]
        outs.append(win / cnt - pf[:, :, sl])
    return jnp.concatenate(outs, axis=-1).astype(p.dtype)


def chunked_sgu(u, v, norm_g, w_s, b_s):
    b, s, _ = u.shape
    v = rms_norm(v, norm_g)
    n = s // SGU_CHUNK
    vc = v.reshape(b, n, SGU_CHUNK, N_SGU_GROUPS, SGU_GROUP)
    mixed = jnp.einsum('gpq,bnqgc->bnpgc', w_s, vc) + b_s.T[None, None, :, :, None]
    return u * mixed.reshape(b, s, SGU_WIDTH)


def odd_mixer(h, w_in, pool_w, pool_scale, sgu_norm, sgu_w, sgu_b, w_out):
    b, s, _ = h.shape
    p, u, v = jnp.split(h @ w_in, list(ODD_SPLITS), axis=-1)
    pooled = multiscale_pool(p).reshape(b, s, N_POOL_GROUPS, POOL_GROUP)
    c_out = jnp.einsum('bsgc,gcd->bsgd', pooled, pool_w).reshape(b, s, POOL_WIDTH) * pool_scale
    d_out = chunked_sgu(jax.nn.gelu(u), jax.nn.gelu(v), sgu_norm, sgu_w, sgu_b)
    return jnp.concatenate([c_out, d_out], axis=-1) @ w_out


def setup_inputs(seed: int = 0) -> dict:
    key = jax.random.key(seed)
    ks = jax.random.split(key, 24)
    n_even = (DEPTH + 1) // 2
    n_odd = DEPTH // 2
    f32 = jnp.float32

    def nrm(k, shape, scale):
        return jax.random.normal(k, shape, f32) * scale

    def gain(k, shape):
        return 1.0 + 0.02 * jax.random.normal(k, shape, f32)

    return {
        'x': jax.random.normal(ks[0], (BATCH, SEQ, D_MODEL), f32),
        'ffn1_norm': gain(ks[1], (DEPTH, D_MODEL)),
        'ffn1_w_in': nrm(ks[2], (DEPTH, D_MODEL, 2 * D_FF), D_MODEL ** -0.5),
        'ffn1_w_out': nrm(ks[3], (DEPTH, D_FF, D_MODEL), D_FF ** -0.5),
        'mix_norm': gain(ks[4], (DEPTH, D_MODEL)),
        'ffn2_norm': gain(ks[5], (DEPTH, D_MODEL)),
        'ffn2_w_in': nrm(ks[6], (DEPTH, D_MODEL, 2 * D_FF), D_MODEL ** -0.5),
        'ffn2_w_out': nrm(ks[7], (DEPTH, D_FF, D_MODEL), D_FF ** -0.5),
        'ev_w_in': nrm(ks[8], (n_even, D_MODEL, EVEN_IN), D_MODEL ** -0.5),
        'ev_conv_w': nrm(ks[9], (n_even, 3, CONV_WIDTH), 3 ** -0.5),
        'ev_q_norm': gain(ks[10], (n_even, HEAD_DIM)),
        'ev_k_norm': gain(ks[11], (n_even, HEAD_DIM)),
        'ev_w_out': nrm(ks[12], (n_even, EVEN_MIX, D_MODEL), EVEN_MIX ** -0.5),
        'od_w_in': nrm(ks[13], (n_odd, D_MODEL, ODD_IN), D_MODEL ** -0.5),
        'od_pool_w': nrm(ks[14], (n_odd, N_POOL_GROUPS, POOL_GROUP, POOL_GROUP), POOL_GROUP ** -0.5),
        'od_pool_scale': 1.0 + 0.1 * jax.random.normal(ks[15], (n_odd, POOL_WIDTH), f32),
        'od_sgu_norm': gain(ks[16], (n_odd, SGU_WIDTH)),
        'od_sgu_w': nrm(ks[17], (n_odd, N_SGU_GROUPS, SGU_CHUNK, SGU_CHUNK), SGU_CHUNK ** -0.5),
        'od_sgu_b': 1.0 + 0.01 * jax.random.normal(ks[18], (n_odd, N_SGU_GROUPS, SGU_CHUNK), f32),
        'od_w_out': nrm(ks[19], (n_odd, ODD_MIX, D_MODEL), ODD_MIX ** -0.5),
        'final_norm': gain(ks[20], (D_MODEL,)),
    }


def reference(x, ffn1_norm, ffn1_w_in, ffn1_w_out, mix_norm, ffn2_norm, ffn2_w_in, ffn2_w_out,
              ev_w_in, ev_conv_w, ev_q_norm, ev_k_norm, ev_w_out,
              od_w_in, od_pool_w, od_pool_scale, od_sgu_norm, od_sgu_w, od_sgu_b, od_w_out,
              final_norm):
    s = x.shape[1]
    cos, sin = axial_rope_tables(s)
    for layer in range(DEPTH):
        x = x + 0.5 * swiglu(rms_norm(x, ffn1_norm[layer]), ffn1_w_in[layer], ffn1_w_out[layer])
        h = rms_norm(x, mix_norm[layer])
        j = layer // 2
        if layer % 2 == 0:
            x = x + even_mixer(h, ev_w_in[j], ev_conv_w[j], ev_q_norm[j], ev_k_norm[j], ev_w_out[j], cos, sin)
        else:
            x = x + odd_mixer(h, od_w_in[j], od_pool_w[j], od_pool_scale[j], od_sgu_norm[j],
                              od_sgu_w[j], od_sgu_b[j], od_w_out[j])
        x = x + 0.5 * swiglu(rms_norm(x, ffn2_norm[layer]), ffn2_w_in[layer], ffn2_w_out[layer])
    return rms_norm(x, final_norm)
```

```python
import functools

import jax
import jax.numpy as jnp
from jax import lax
from jax.experimental import pallas as pl
from jax.experimental.pallas import tpu as pltpu

F32 = jnp.float32
BF16 = jnp.bfloat16

GRID_W = 64
HEAD_DIM = 64
N_Q_HEADS = 8
N_KV_HEADS = 2
Q_PER_KV = N_Q_HEADS // N_KV_HEADS
ROPE_THETA = 10000.0
POOL_RADII = (1, 2, 4, 8)
POOL_HALO = 8
N_GROUPS = 4
GROUP = 128
EPS = 1e-6

LANES = 128
SUBLANES = 8

FFN_ROWS = 512
FFN_CHUNKS = 2
MIX_ROWS = 512
VMEM_LIMIT_FFN = 56 * 1024 * 1024
VMEM_LIMIT_MIX = 48 * 1024 * 1024

HEAD_ORDER = tuple(g * Q_PER_KV + j for j in range(Q_PER_KV) for g in range(N_KV_HEADS))


def _rms(xf, g):
    return xf * lax.rsqrt(jnp.mean(xf * xf, axis=-1, keepdims=True) + EPS) * g


def _dot(a, b):
    return jnp.dot(a, b, preferred_element_type=F32)


def _const_spec(shape):
    zeros = (0,) * len(shape)
    return pl.BlockSpec(shape, lambda *_: zeros)


def _ffn_kernel(x_ref, g_ref, win_ref, wout_ref, fin_ref, o_ref, *, n_chunks, final):
    x = x_ref[...]
    h = _rms(x, g_ref[...]).astype(BF16)
    cf = wout_ref.shape[0] // n_chunks
    y = None
    for c in range(n_chunks):
        gu = _dot(h, win_ref[:, 2 * cf * c:2 * cf * (c + 1)])
        g = gu[:, :cf]
        u = gu[:, cf:]
        a = (g * jax.nn.sigmoid(g) * u).astype(BF16)
        d = _dot(a, wout_ref[cf * c:cf * (c + 1), :])
        y = d if y is None else y + d
    out = x + 0.5 * y
    if final:
        out = _rms(out, fin_ref[...])
    o_ref[...] = out


def _ffn(x2, gain, w_in, w_out, final_gain, final):
    n, d = x2.shape
    tm = min(FFN_ROWS, n)
    kern = functools.partial(_ffn_kernel, n_chunks=FFN_CHUNKS, final=final)
    return pl.pallas_call(
        kern,
        grid=(n // tm,),
        in_specs=[
            pl.BlockSpec((tm, d), lambda i: (i, 0)),
            _const_spec((1, d)),
            _const_spec(w_in.shape),
            _const_spec(w_out.shape),
            _const_spec((1, d)),
        ],
        out_specs=pl.BlockSpec((tm, d), lambda i: (i, 0)),
        out_shape=jax.ShapeDtypeStruct((n, d), F32),
        compiler_params=pltpu.CompilerParams(
            dimension_semantics=("parallel",), vmem_limit_bytes=VMEM_LIMIT_FFN),
        name="ffn_final" if final else "ffn",
    )(x2, gain, w_in, w_out, final_gain)


def _prep_ffn_w_in(w_in):
    d, two_ff = w_in.shape
    cf = two_ff // 2 // FFN_CHUNKS
    w = w_in.reshape(d, 2, FFN_CHUNKS, cf).transpose(0, 2, 1, 3).reshape(d, two_ff)
    return w.astype(BF16)


def _even_in_kernel(x_ref, g_ref, win_ref, qg_ref, kg_ref, cos_ref, sin_ref, bd_ref,
                    gb_ref, gc_ref, q_ref, kt0_ref, kt1_ref, v_ref):
    x = x_ref[0]
    h = _rms(x, g_ref[...]).astype(BF16)
    proj = _dot(h, win_ref[...])
    cw = gb_ref.shape[2]
    gb_ref[0] = proj[:, :cw]
    gc_ref[0] = proj[:, cw:2 * cw] * proj[:, 2 * cw:3 * cw]
    qk0 = 3 * cw
    n_q = q_ref.shape[2] // LANES
    cos = cos_ref[...]
    sin = sin_ref[...]
    bd = bd_ref[...]
    lane = lax.broadcasted_iota(jnp.int32, cos.shape, 1)
    even = (lane & 1) == 0
    for j in range(n_q + 1):
        c = proj[:, qk0 + LANES * j:qk0 + LANES * (j + 1)]
        sq = c * c
        hi = sq.astype(BF16)
        lo = (sq - hi.astype(F32)).astype(BF16)
        ms = (_dot(hi, bd) + _dot(lo, bd)) * (1.0 / HEAD_DIM)
        gain = qg_ref[...] if j < n_q else kg_ref[...]
        y = c * lax.rsqrt(ms + EPS) * gain
        partner = jnp.where(even, pltpu.roll(y, LANES - 1, 1), pltpu.roll(y, 1, 1))
        r = y * cos + partner * sin
        if j < n_q:
            q_ref[0, :, LANES * j:LANES * (j + 1)] = (r * (HEAD_DIM ** -0.5)).astype(BF16)
        else:
            kt = r.T
            row = lax.broadcasted_iota(jnp.int32, kt.shape, 0)
            kt0_ref[0] = jnp.where(row < HEAD_DIM, kt, 0.0).astype(BF16)
            kt1_ref[0] = jnp.where(row >= HEAD_DIM, kt, 0.0).astype(BF16)
    v0 = qk0 + LANES * (n_q + 1)
    v_ref[0] = proj[:, v0:].astype(BF16)


def _even_out_kernel(x_ref, gb_ref, gc_ref, gcp_ref, gcn_ref, cw_ref, q_ref, kt0_ref, kt1_ref,
                     v_ref, wa_ref, wb_ref, o_ref):
    i = pl.program_id(1)
    last = pl.num_programs(1) - 1
    gc = gc_ref[0]
    tq = gc.shape[0]
    prev_row = jnp.where(i > 0, gcp_ref[0, SUBLANES - 1:SUBLANES, :], 0.0)
    next_row = jnp.where(i < last, gcn_ref[0, 0:1, :], 0.0)
    row = lax.broadcasted_iota(jnp.int32, gc.shape, 0)
    gc_m1 = jnp.where(row == 0, prev_row, pltpu.roll(gc, 1, 0))
    gc_p1 = jnp.where(row == tq - 1, next_row, pltpu.roll(gc, tq - 1, 0))
    cw = cw_ref[...]
    a_out = gb_ref[0] * (gc_m1 * cw[0:1, :] + gc * cw[1:2, :] + gc_p1 * cw[2:3, :])

    q = q_ref[0]
    v = v_ref[0]
    lane = lax.broadcasted_iota(jnp.int32, (tq, LANES), 1)
    pairs = []
    for j in range(q.shape[1] // LANES):
        qj = q[:, LANES * j:LANES * (j + 1)]
        outs = []
        for kt_ref in (kt0_ref, kt1_ref):
            s = _dot(qj, kt_ref[0])
            m = jnp.max(s, axis=-1, keepdims=True)
            p = jnp.exp(s - m)
            l = jnp.sum(p, axis=-1, keepdims=True)
            outs.append(_dot(p.astype(BF16), v) / l)
        pairs.append(jnp.where(lane < HEAD_DIM, outs[0], outs[1]))
    b_out = jnp.concatenate(pairs, axis=1)
    o_ref[0] = (x_ref[0] + _dot(a_out.astype(BF16), wa_ref[...])
                + _dot(b_out.astype(BF16), wb_ref[...]))


def _rope_tables(seq):
    rows = seq // GRID_W
    r_idx, c_idx = jnp.meshgrid(jnp.arange(rows), jnp.arange(GRID_W), indexing='ij')
    r_idx = r_idx.reshape(-1).astype(F32)
    c_idx = c_idx.reshape(-1).astype(F32)
    n_freq = HEAD_DIM // 4
    inv = ROPE_THETA ** (-jnp.arange(n_freq, dtype=F32) / n_freq)
    ang = jnp.concatenate([r_idx[:, None] * inv, c_idx[:, None] * inv], axis=-1)
    cos = jnp.repeat(jnp.cos(ang), 2, axis=-1)
    sign = jnp.tile(jnp.array([-1.0, 1.0], F32), HEAD_DIM // 2)
    sin = jnp.repeat(jnp.sin(ang), 2, axis=-1) * sign
    reps = LANES // HEAD_DIM
    return jnp.tile(cos, (1, reps)), jnp.tile(sin, (1, reps))


def _even_mixer(x, gain, w_in, conv_w, q_g, k_g, w_out, cos, sin):
    b, s, d = x.shape
    cw = conv_w.shape[1]
    aw = N_Q_HEADS * HEAD_DIM
    kw = N_KV_HEADS * HEAD_DIM
    ts = min(MIX_ROWS, s)
    nt = s // ts

    q0 = 3 * cw
    wq = w_in[:, q0:q0 + aw].reshape(d, N_Q_HEADS, HEAD_DIM)[:, HEAD_ORDER, :].reshape(d, aw)
    w_in_p = jnp.concatenate([w_in[:, :q0], wq, w_in[:, q0 + aw:]], axis=1).astype(BF16)
    reps = LANES // HEAD_DIM
    qg = jnp.tile(q_g, reps)[None, :]
    kg = jnp.tile(k_g, reps)[None, :]
    lane = jnp.arange(LANES)
    bd = (lane[:, None] // HEAD_DIM == lane[None, :] // HEAD_DIM).astype(BF16)

    row_spec = lambda w: pl.BlockSpec((1, ts, w), lambda bi, ti: (bi, ti, 0))
    gb, gc, qp, kt0, kt1, vv = pl.pallas_call(
        _even_in_kernel,
        grid=(b, nt),
        in_specs=[
            row_spec(d),
            _const_spec((1, d)),
            _const_spec(w_in_p.shape),
            _const_spec((1, LANES)),
            _const_spec((1, LANES)),
            pl.BlockSpec((ts, LANES), lambda bi, ti: (ti, 0)),
            pl.BlockSpec((ts, LANES), lambda bi, ti: (ti, 0)),
            _const_spec((LANES, LANES)),
        ],
        out_specs=[
            row_spec(cw), row_spec(cw), row_spec(aw),
            pl.BlockSpec((1, kw, ts), lambda bi, ti: (bi, 0, ti)),
            pl.BlockSpec((1, kw, ts), lambda bi, ti: (bi, 0, ti)),
            row_spec(kw),
        ],
        out_shape=[
            jax.ShapeDtypeStruct((b, s, cw), F32),
            jax.ShapeDtypeStruct((b, s, cw), F32),
            jax.ShapeDtypeStruct((b, s, aw), BF16),
            jax.ShapeDtypeStruct((b, kw, s), BF16),
            jax.ShapeDtypeStruct((b, kw, s), BF16),
            jax.ShapeDtypeStruct((b, s, kw), BF16),
        ],
        compiler_params=pltpu.CompilerParams(
            dimension_semantics=("parallel", "parallel"), vmem_limit_bytes=VMEM_LIMIT_MIX),
        name="even_in",
    )(x, gain, w_in_p, qg, kg, cos, sin, bd)

    wa = w_out[:cw].astype(BF16)
    wb = w_out[cw:].reshape(N_Q_HEADS, HEAD_DIM, d)[HEAD_ORDER, :, :].reshape(aw, d).astype(BF16)
    hb = ts // SUBLANES
    n_hb = s // SUBLANES
    return pl.pallas_call(
        _even_out_kernel,
        grid=(b, nt),
        in_specs=[
            row_spec(d), row_spec(cw), row_spec(cw),
            pl.BlockSpec((1, SUBLANES, cw), lambda bi, ti: (bi, jnp.maximum(ti * hb - 1, 0), 0)),
            pl.BlockSpec((1, SUBLANES, cw),
                         lambda bi, ti: (bi, jnp.minimum((ti + 1) * hb, n_hb - 1), 0)),
            _const_spec(conv_w.shape),
            row_spec(aw),
            pl.BlockSpec((1, kw, s), lambda bi, ti: (bi, 0, 0)),
            pl.BlockSpec((1, kw, s), lambda bi, ti: (bi, 0, 0)),
            pl.BlockSpec((1, s, kw), lambda bi, ti: (bi, 0, 0)),
            _const_spec(wa.shape), _const_spec(wb.shape),
        ],
        out_specs=row_spec(d),
        out_shape=jax.ShapeDtypeStruct((b, s, d), F32),
        compiler_params=pltpu.CompilerParams(
            dimension_semantics=("parallel", "arbitrary"), vmem_limit_bytes=VMEM_LIMIT_MIX),
        name="even_out",
    )(x, gb, gc, gc, gc, conv_w, qp, kt0, kt1, vv, wa, wb)


def _odd_in_kernel(x_ref, g_ref, win_ref, ng_ref, p_ref, u_ref, v_ref):
    x = x_ref[0]
    h = _rms(x, g_ref[...]).astype(BF16)
    proj = _dot(h, win_ref[...])
    pw = p_ref.shape[2]
    sw = u_ref.shape[2]
    p_ref[0] = proj[:, :pw]
    u_ref[0] = jax.nn.gelu(proj[:, pw:pw + sw]).astype(BF16)
    vg = jax.nn.gelu(proj[:, pw + sw:])
    v_ref[0] = _rms(vg, ng_ref[...]).astype(BF16)


def _odd_out_kernel(x_ref, p_ref, pp_ref, pn_ref, u_ref, v_ref, pbd_ref, ps_ref, ws_ref, bs_ref,
                    wc_ref, wd_ref, o_ref, ext_ref, *, seq):
    i = pl.program_id(1)
    last = pl.num_programs(1) - 1
    p = p_ref[0]
    ts = p.shape[0]
    ext_ref[0:POOL_HALO, :] = jnp.where(i > 0, pp_ref[0], 0.0)
    ext_ref[POOL_HALO:POOL_HALO + ts, :] = p
    ext_ref[POOL_HALO + ts:, :] = jnp.where(i < last, pn_ref[0], 0.0)
    t = i * ts + lax.broadcasted_iota(jnp.int32, (ts, 1), 0)
    pooled = []
    for g, r in enumerate(POOL_RADII):
        cols = slice(GROUP * g, GROUP * (g + 1))
        win = None
        for dlt in range(-r, r + 1):
            term = ext_ref[pl.ds(POOL_HALO + dlt, ts), cols]
            win = term if win is None else win + term
        cnt = (jnp.minimum(t + r, seq - 1) - jnp.maximum(t - r, 0) + 1).astype(F32)
        pooled.append(win / cnt - p[:, cols])
    pooled = jnp.concatenate(pooled, axis=1).astype(BF16)
    c_out = _dot(pooled, pbd_ref[...]) * ps_ref[...]

    u = u_ref[0].astype(F32)
    v = v_ref[0]
    rows = []
    for n in range(ts // GROUP):
        rs = slice(GROUP * n, GROUP * (n + 1))
        blocks = []
        for g in range(N_GROUPS):
            cs = slice(GROUP * g, GROUP * (g + 1))
            mixed = _dot(ws_ref[g], v[rs, cs]) + bs_ref[g]
            blocks.append(u[rs, cs] * mixed)
        rows.append(jnp.concatenate(blocks, axis=1))
    d_out = jnp.concatenate(rows, axis=0)
    o_ref[0] = (x_ref[0] + _dot(c_out.astype(BF16), wc_ref[...])
                + _dot(d_out.astype(BF16), wd_ref[...]))


def _odd_mixer(x, gain, w_in, pool_w, pool_scale, sgu_norm, sgu_w, sgu_b, w_out):
    b, s, d = x.shape
    pw = pool_scale.shape[0]
    sw = sgu_norm.shape[0]
    ts = min(MIX_ROWS, s)
    nt = s // ts
    row_spec = lambda w: pl.BlockSpec((1, ts, w), lambda bi, ti: (bi, ti, 0))

    w_in_b = w_in.astype(BF16)
    p, ug, vn = pl.pallas_call(
        _odd_in_kernel,
        grid=(b, nt),
        in_specs=[row_spec(d), _const_spec((1, d)), _const_spec(w_in_b.shape),
                  _const_spec((1, sw))],
        out_specs=[row_spec(pw), row_spec(sw), row_spec(sw)],
        out_shape=[
            jax.ShapeDtypeStruct((b, s, pw), F32),
            jax.ShapeDtypeStruct((b, s, sw), BF16),
            jax.ShapeDtypeStruct((b, s, sw), BF16),
        ],
        compiler_params=pltpu.CompilerParams(
            dimension_semantics=("parallel", "parallel"), vmem_limit_bytes=VMEM_LIMIT_MIX),
        name="odd_in",
    )(x, gain, w_in_b, sgu_norm[None, :])

    pbd = jax.scipy.linalg.block_diag(*[pool_w[g] for g in range(N_GROUPS)]).astype(BF16)
    bs = jnp.broadcast_to(sgu_b[:, :, None], sgu_b.shape + (GROUP,))
    wc = w_out[:pw].astype(BF16)
    wd = w_out[pw:].astype(BF16)
    hb = ts // POOL_HALO
    n_hb = s // POOL_HALO
    kern = functools.partial(_odd_out_kernel, seq=s)
    return pl.pallas_call(
        kern,
        grid=(b, nt),
        in_specs=[
            row_spec(d), row_spec(pw),
            pl.BlockSpec((1, POOL_HALO, pw), lambda bi, ti: (bi, jnp.maximum(ti * hb - 1, 0), 0)),
            pl.BlockSpec((1, POOL_HALO, pw),
                         lambda bi, ti: (bi, jnp.minimum((ti + 1) * hb, n_hb - 1), 0)),
            row_spec(sw), row_spec(sw),
            _const_spec(pbd.shape), _const_spec((1, pw)),
            _const_spec(sgu_w.shape), _const_spec(bs.shape),
            _const_spec(wc.shape), _const_spec(wd.shape),
        ],
        out_specs=row_spec(d),
        out_shape=jax.ShapeDtypeStruct((b, s, d), F32),
        scratch_shapes=[pltpu.VMEM((ts + 2 * POOL_HALO, pw), F32)],
        compiler_params=pltpu.CompilerParams(
            dimension_semantics=("parallel", "arbitrary"), vmem_limit_bytes=VMEM_LIMIT_MIX),
        name="odd_out",
    )(x, p, p, p, ug, vn, pbd, pool_scale[None, :], sgu_w.astype(BF16), bs, wc, wd)


def kernel(x, ffn1_norm, ffn1_w_in, ffn1_w_out, mix_norm, ffn2_norm, ffn2_w_in, ffn2_w_out,
           ev_w_in, ev_conv_w, ev_q_norm, ev_k_norm, ev_w_out,
           od_w_in, od_pool_w, od_pool_scale, od_sgu_norm, od_sgu_w, od_sgu_b, od_w_out,
           final_norm):
    b, s, d = x.shape
    depth = ffn1_norm.shape[0]
    cos, sin = _rope_tables(s)
    fin = final_norm[None, :]
    for layer in range(depth):
        x = _ffn(x.reshape(b * s, d), ffn1_norm[layer][None, :], _prep_ffn_w_in(ffn1_w_in[layer]),
                 ffn1_w_out[layer].astype(BF16), fin, False).reshape(b, s, d)
        j = layer // 2
        gain = mix_norm[layer][None, :]
        if layer % 2 == 0:
            x = _even_mixer(x, gain, ev_w_in[j], ev_conv_w[j], ev_q_norm[j], ev_k_norm[j],
                            ev_w_out[j], cos, sin)
        else:
            x = _odd_mixer(x, gain, od_w_in[j], od_pool_w[j], od_pool_scale[j], od_sgu_norm[j],
                           od_sgu_w[j], od_sgu_b[j], od_w_out[j])
        x = _ffn(x.reshape(b * s, d), ffn2_norm[layer][None, :], _prep_ffn_w_in(ffn2_w_in[layer]),
                 ffn2_w_out[layer].astype(BF16), fin, layer == depth - 1).reshape(b, s, d)
    return x
```

```python
import functools

import jax
import jax.numpy as jnp
from jax import lax
from jax.experimental import pallas as pl
from jax.experimental.pallas import tpu as pltpu

F32 = jnp.float32
BF16 = jnp.bfloat16

GRID_W = 64
HEAD_DIM = 64
N_Q_HEADS = 8
N_KV_HEADS = 2
Q_PER_KV = N_Q_HEADS // N_KV_HEADS
ROPE_THETA = 10000.0
POOL_RADII = (1, 2, 4, 8)
POOL_HALO = 8
N_GROUPS = 4
GROUP = 128
EPS = 1e-6

LANES = 128
SUBLANES = 8

FFN_ROWS = 512
MIX_ROWS = 512
VMEM_LIMIT_FFN = 56 * 1024 * 1024
VMEM_LIMIT_MIX = 48 * 1024 * 1024

HEAD_ORDER = tuple(g * Q_PER_KV + j for j in range(Q_PER_KV) for g in range(N_KV_HEADS))


def _rms(xf, g):
    return xf * lax.rsqrt(jnp.mean(xf * xf, axis=-1, keepdims=True) + EPS) * g


def _dot(a, b):
    return jnp.dot(a, b, preferred_element_type=F32)


def _const_spec(shape):
    zeros = (0,) * len(shape)
    return pl.BlockSpec(shape, lambda *_: zeros)


def _ffn_kernel(x_ref, g_ref, win_ref, wout_ref, fin_ref, o_ref, *, final):
    x = x_ref[...]
    h = _rms(x, g_ref[...]).astype(BF16)
    d_ff = wout_ref.shape[0]
    gu = _dot(h, win_ref[...])
    g = gu[:, :d_ff]
    u = gu[:, d_ff:]
    a = (g * jax.nn.sigmoid(g) * u).astype(BF16)
    out = x + 0.5 * _dot(a, wout_ref[...])
    if final:
        out = _rms(out, fin_ref[...])
    o_ref[...] = out


def _ffn(x2, layer, gains, w_in, w_out, final_gain, final):
    n, d = x2.shape
    tm = min(FFN_ROWS, n)
    layer_spec = lambda a: pl.BlockSpec((None,) + a.shape[1:], lambda i: (layer, 0, 0))
    return pl.pallas_call(
        functools.partial(_ffn_kernel, final=final),
        grid=(n // tm,),
        in_specs=[
            pl.BlockSpec((tm, d), lambda i: (i, 0)),
            layer_spec(gains), layer_spec(w_in), layer_spec(w_out),
            _const_spec((1, d)),
        ],
        out_specs=pl.BlockSpec((tm, d), lambda i: (i, 0)),
        out_shape=jax.ShapeDtypeStruct((n, d), F32),
        compiler_params=pltpu.CompilerParams(
            dimension_semantics=("parallel",), vmem_limit_bytes=VMEM_LIMIT_FFN),
        name="ffn_final" if final else "ffn",
    )(x2, gains, w_in, w_out, final_gain)


def _even_in_kernel(x_ref, g_ref, win_ref, qg_ref, kg_ref, cos_ref, sin_ref, bd_ref,
                    gb_ref, gc_ref, q_ref, kt0_ref, kt1_ref, v_ref):
    x = x_ref[0]
    h = _rms(x, g_ref[...]).astype(BF16)
    proj = _dot(h, win_ref[...])
    cw = gb_ref.shape[2]
    gb_ref[0] = proj[:, :cw]
    gc_ref[0] = proj[:, cw:2 * cw] * proj[:, 2 * cw:3 * cw]
    qk0 = 3 * cw
    n_q = q_ref.shape[2] // LANES
    cos = cos_ref[...]
    sin = sin_ref[...]
    bd = bd_ref[...]
    lane = lax.broadcasted_iota(jnp.int32, cos.shape, 1)
    even = (lane & 1) == 0
    for j in range(n_q + 1):
        c = proj[:, qk0 + LANES * j:qk0 + LANES * (j + 1)]
        sq = c * c
        hi = sq.astype(BF16)
        lo = (sq - hi.astype(F32)).astype(BF16)
        ms = (_dot(hi, bd) + _dot(lo, bd)) * (1.0 / HEAD_DIM)
        gain = qg_ref[...] if j < n_q else kg_ref[...]
        y = c * lax.rsqrt(ms + EPS) * gain
        partner = jnp.where(even, pltpu.roll(y, LANES - 1, 1), pltpu.roll(y, 1, 1))
        r = y * cos + partner * sin
        if j < n_q:
            q_ref[0, :, LANES * j:LANES * (j + 1)] = (r * (HEAD_DIM ** -0.5)).astype(BF16)
        else:
            kt = r.T
            row = lax.broadcasted_iota(jnp.int32, kt.shape, 0)
            kt0_ref[0] = jnp.where(row < HEAD_DIM, kt, 0.0).astype(BF16)
            kt1_ref[0] = jnp.where(row >= HEAD_DIM, kt, 0.0).astype(BF16)
    v0 = qk0 + LANES * (n_q + 1)
    v_ref[0] = proj[:, v0:].astype(BF16)


def _even_out_kernel(x_ref, gb_ref, gc_ref, gcp_ref, gcn_ref, cw_ref, q_ref, kt0_ref, kt1_ref,
                     v_ref, wa_ref, wb_ref, o_ref):
    i = pl.program_id(1)
    last = pl.num_programs(1) - 1
    gc = gc_ref[0]
    tq = gc.shape[0]
    prev_row = jnp.where(i > 0, gcp_ref[0, SUBLANES - 1:SUBLANES, :], 0.0)
    next_row = jnp.where(i < last, gcn_ref[0, 0:1, :], 0.0)
    row = lax.broadcasted_iota(jnp.int32, gc.shape, 0)
    gc_m1 = jnp.where(row == 0, prev_row, pltpu.roll(gc, 1, 0))
    gc_p1 = jnp.where(row == tq - 1, next_row, pltpu.roll(gc, tq - 1, 0))
    cw = cw_ref[...]
    a_out = gb_ref[0] * (gc_m1 * cw[0:1, :] + gc * cw[1:2, :] + gc_p1 * cw[2:3, :])

    q = q_ref[0]
    v = v_ref[0]
    lane = lax.broadcasted_iota(jnp.int32, (tq, LANES), 1)
    pairs = []
    for j in range(q.shape[1] // LANES):
        qj = q[:, LANES * j:LANES * (j + 1)]
        outs = []
        for kt_ref in (kt0_ref, kt1_ref):
            s = _dot(qj, kt_ref[0])
            m = jnp.max(s, axis=-1, keepdims=True)
            p = jnp.exp(s - m)
            l = jnp.sum(p, axis=-1, keepdims=True)
            outs.append(_dot(p.astype(BF16), v) / l)
        pairs.append(jnp.where(lane < HEAD_DIM, outs[0], outs[1]))
    b_out = jnp.concatenate(pairs, axis=1)
    o_ref[0] = (x_ref[0] + _dot(a_out.astype(BF16), wa_ref[...])
                + _dot(b_out.astype(BF16), wb_ref[...]))


def _rope_tables(seq):
    rows = seq // GRID_W
    r_idx, c_idx = jnp.meshgrid(jnp.arange(rows), jnp.arange(GRID_W), indexing='ij')
    r_idx = r_idx.reshape(-1).astype(F32)
    c_idx = c_idx.reshape(-1).astype(F32)
    n_freq = HEAD_DIM // 4
    inv = ROPE_THETA ** (-jnp.arange(n_freq, dtype=F32) / n_freq)
    ang = jnp.concatenate([r_idx[:, None] * inv, c_idx[:, None] * inv], axis=-1)
    cos = jnp.repeat(jnp.cos(ang), 2, axis=-1)
    sign = jnp.tile(jnp.array([-1.0, 1.0], F32), HEAD_DIM // 2)
    sin = jnp.repeat(jnp.sin(ang), 2, axis=-1) * sign
    reps = LANES // HEAD_DIM
    return jnp.tile(cos, (1, reps)), jnp.tile(sin, (1, reps))


def _even_mixer(x, gain, w_in, conv_w, q_g, k_g, w_out, cos, sin):
    b, s, d = x.shape
    cw = conv_w.shape[1]
    aw = N_Q_HEADS * HEAD_DIM
    kw = N_KV_HEADS * HEAD_DIM
    ts = min(MIX_ROWS, s)
    nt = s // ts

    q0 = 3 * cw
    wq = w_in[:, q0:q0 + aw].reshape(d, N_Q_HEADS, HEAD_DIM)[:, HEAD_ORDER, :].reshape(d, aw)
    w_in_p = jnp.concatenate([w_in[:, :q0], wq, w_in[:, q0 + aw:]], axis=1).astype(BF16)
    reps = LANES // HEAD_DIM
    qg = jnp.tile(q_g, reps)[None, :]
    kg = jnp.tile(k_g, reps)[None, :]
    lane = jnp.arange(LANES)
    bd = (lane[:, None] // HEAD_DIM == lane[None, :] // HEAD_DIM).astype(BF16)

    row_spec = lambda w: pl.BlockSpec((1, ts, w), lambda bi, ti: (bi, ti, 0))
    gb, gc, qp, kt0, kt1, vv = pl.pallas_call(
        _even_in_kernel,
        grid=(b, nt),
        in_specs=[
            row_spec(d),
            _const_spec((1, d)),
            _const_spec(w_in_p.shape),
            _const_spec((1, LANES)),
            _const_spec((1, LANES)),
            pl.BlockSpec((ts, LANES), lambda bi, ti: (ti, 0)),
            pl.BlockSpec((ts, LANES), lambda bi, ti: (ti, 0)),
            _const_spec((LANES, LANES)),
        ],
        out_specs=[
            row_spec(cw), row_spec(cw), row_spec(aw),
            pl.BlockSpec((1, kw, ts), lambda bi, ti: (bi, 0, ti)),
            pl.BlockSpec((1, kw, ts), lambda bi, ti: (bi, 0, ti)),
            row_spec(kw),
        ],
        out_shape=[
            jax.ShapeDtypeStruct((b, s, cw), F32),
            jax.ShapeDtypeStruct((b, s, cw), F32),
            jax.ShapeDtypeStruct((b, s, aw), BF16),
            jax.ShapeDtypeStruct((b, kw, s), BF16),
            jax.ShapeDtypeStruct((b, kw, s), BF16),
            jax.ShapeDtypeStruct((b, s, kw), BF16),
        ],
        compiler_params=pltpu.CompilerParams(
            dimension_semantics=("parallel", "parallel"), vmem_limit_bytes=VMEM_LIMIT_MIX),
        name="even_in",
    )(x, gain, w_in_p, qg, kg, cos, sin, bd)

    wa = w_out[:cw].astype(BF16)
    wb = w_out[cw:].reshape(N_Q_HEADS, HEAD_DIM, d)[HEAD_ORDER, :, :].reshape(aw, d).astype(BF16)
    hb = ts // SUBLANES
    n_hb = s // SUBLANES
    return pl.pallas_call(
        _even_out_kernel,
        grid=(b, nt),
        in_specs=[
            row_spec(d), row_spec(cw), row_spec(cw),
            pl.BlockSpec((1, SUBLANES, cw), lambda bi, ti: (bi, jnp.maximum(ti * hb - 1, 0), 0)),
            pl.BlockSpec((1, SUBLANES, cw),
                         lambda bi, ti: (bi, jnp.minimum((ti + 1) * hb, n_hb - 1), 0)),
            _const_spec(conv_w.shape),
            row_spec(aw),
            pl.BlockSpec((1, kw, s), lambda bi, ti: (bi, 0, 0)),
            pl.BlockSpec((1, kw, s), lambda bi, ti: (bi, 0, 0)),
            pl.BlockSpec((1, s, kw), lambda bi, ti: (bi, 0, 0)),
            _const_spec(wa.shape), _const_spec(wb.shape),
        ],
        out_specs=row_spec(d),
        out_shape=jax.ShapeDtypeStruct((b, s, d), F32),
        compiler_params=pltpu.CompilerParams(
            dimension_semantics=("parallel", "arbitrary"), vmem_limit_bytes=VMEM_LIMIT_MIX),
        name="even_out",
    )(x, gb, gc, gc, gc, conv_w, qp, kt0, kt1, vv, wa, wb)


def _odd_in_kernel(x_ref, g_ref, win_ref, ng_ref, p_ref, u_ref, v_ref):
    x = x_ref[0]
    h = _rms(x, g_ref[...]).astype(BF16)
    proj = _dot(h, win_ref[...])
    pw = p_ref.shape[2]
    sw = u_ref.shape[2]
    p_ref[0] = proj[:, :pw]
    u_ref[0] = jax.nn.gelu(proj[:, pw:pw + sw]).astype(BF16)
    vg = jax.nn.gelu(proj[:, pw + sw:])
    v_ref[0] = _rms(vg, ng_ref[...]).astype(BF16)


def _odd_out_kernel(x_ref, p_ref, pp_ref, pn_ref, u_ref, v_ref, pbd_ref, ps_ref, ws_ref, bs_ref,
                    wc_ref, wd_ref, o_ref, ext_ref, *, seq):
    i = pl.program_id(1)
    last = pl.num_programs(1) - 1
    p = p_ref[0]
    ts = p.shape[0]
    ext_ref[0:POOL_HALO, :] = jnp.where(i > 0, pp_ref[0], 0.0)
    ext_ref[POOL_HALO:POOL_HALO + ts, :] = p
    ext_ref[POOL_HALO + ts:, :] = jnp.where(i < last, pn_ref[0], 0.0)
    t = i * ts + lax.broadcasted_iota(jnp.int32, (ts, 1), 0)
    pooled = []
    for g, r in enumerate(POOL_RADII):
        cols = slice(GROUP * g, GROUP * (g + 1))
        win = None
        for dlt in range(-r, r + 1):
            term = ext_ref[pl.ds(POOL_HALO + dlt, ts), cols]
            win = term if win is None else win + term
        cnt = (jnp.minimum(t + r, seq - 1) - jnp.maximum(t - r, 0) + 1).astype(F32)
        pooled.append(win / cnt - p[:, cols])
    pooled = jnp.concatenate(pooled, axis=1).astype(BF16)
    c_out = _dot(pooled, pbd_ref[...]) * ps_ref[...]

    u = u_ref[0].astype(F32)
    v = v_ref[0]
    rows = []
    for n in range(ts // GROUP):
        rs = slice(GROUP * n, GROUP * (n + 1))
        blocks = []
        for g in range(N_GROUPS):
            cs = slice(GROUP * g, GROUP * (g + 1))
            mixed = _dot(ws_ref[g], v[rs, cs]) + bs_ref[g]
            blocks.append(u[rs, cs] * mixed)
        rows.append(jnp.concatenate(blocks, axis=1))
    d_out = jnp.concatenate(rows, axis=0)
    o_ref[0] = (x_ref[0] + _dot(c_out.astype(BF16), wc_ref[...])
                + _dot(d_out.astype(BF16), wd_ref[...]))


def _odd_mixer(x, gain, w_in, pool_w, pool_scale, sgu_norm, sgu_w, sgu_b, w_out):
    b, s, d = x.shape
    pw = pool_scale.shape[0]
    sw = sgu_norm.shape[0]
    ts = min(MIX_ROWS, s)
    nt = s // ts
    row_spec = lambda w: pl.BlockSpec((1, ts, w), lambda bi, ti: (bi, ti, 0))

    w_in_b = w_in.astype(BF16)
    p, ug, vn = pl.pallas_call(
        _odd_in_kernel,
        grid=(b, nt),
        in_specs=[row_spec(d), _const_spec((1, d)), _const_spec(w_in_b.shape),
                  _const_spec((1, sw))],
        out_specs=[row_spec(pw), row_spec(sw), row_spec(sw)],
        out_shape=[
            jax.ShapeDtypeStruct((b, s, pw), F32),
            jax.ShapeDtypeStruct((b, s, sw), BF16),
            jax.ShapeDtypeStruct((b, s, sw), BF16),
        ],
        compiler_params=pltpu.CompilerParams(
            dimension_semantics=("parallel", "parallel"), vmem_limit_bytes=VMEM_LIMIT_MIX),
        name="odd_in",
    )(x, gain, w_in_b, sgu_norm[None, :])

    pbd = jax.scipy.linalg.block_diag(*[pool_w[g] for g in range(N_GROUPS)]).astype(BF16)
    bs = jnp.broadcast_to(sgu_b[:, :, None], sgu_b.shape + (GROUP,))
    wc = w_out[:pw].astype(BF16)
    wd = w_out[pw:].astype(BF16)
    hb = ts // POOL_HALO
    n_hb = s // POOL_HALO
    kern = functools.partial(_odd_out_kernel, seq=s)
    return pl.pallas_call(
        kern,
        grid=(b, nt),
        in_specs=[
            row_spec(d), row_spec(pw),
            pl.BlockSpec((1, POOL_HALO, pw), lambda bi, ti: (bi, jnp.maximum(ti * hb - 1, 0), 0)),
            pl.BlockSpec((1, POOL_HALO, pw),
                         lambda bi, ti: (bi, jnp.minimum((ti + 1) * hb, n_hb - 1), 0)),
            row_spec(sw), row_spec(sw),
            _const_spec(pbd.shape), _const_spec((1, pw)),
            _const_spec(sgu_w.shape), _const_spec(bs.shape),
            _const_spec(wc.shape), _const_spec(wd.shape),
        ],
        out_specs=row_spec(d),
        out_shape=jax.ShapeDtypeStruct((b, s, d), F32),
        scratch_shapes=[pltpu.VMEM((ts + 2 * POOL_HALO, pw), F32)],
        compiler_params=pltpu.CompilerParams(
            dimension_semantics=("parallel", "arbitrary"), vmem_limit_bytes=VMEM_LIMIT_MIX),
        name="odd_out",
    )(x, p, p, p, ug, vn, pbd, pool_scale[None, :], sgu_w.astype(BF16), bs, wc, wd)


def kernel(x, ffn1_norm, ffn1_w_in, ffn1_w_out, mix_norm, ffn2_norm, ffn2_w_in, ffn2_w_out,
           ev_w_in, ev_conv_w, ev_q_norm, ev_k_norm, ev_w_out,
           od_w_in, od_pool_w, od_pool_scale, od_sgu_norm, od_sgu_w, od_sgu_b, od_w_out,
           final_norm):
    b, s, d = x.shape
    depth = ffn1_norm.shape[0]
    cos, sin = _rope_tables(s)
    fin = final_norm[None, :]
    ffn1 = (ffn1_norm[:, None, :], ffn1_w_in.astype(BF16), ffn1_w_out.astype(BF16))
    ffn2 = (ffn2_norm[:, None, :], ffn2_w_in.astype(BF16), ffn2_w_out.astype(BF16))
    for layer in range(depth):
        x = _ffn(x.reshape(b * s, d), layer, *ffn1, fin, False).reshape(b, s, d)
        j = layer // 2
        gain = mix_norm[layer][None, :]
        if layer % 2 == 0:
            x = _even_mixer(x, gain, ev_w_in[j], ev_conv_w[j], ev_q_norm[j], ev_k_norm[j],
                            ev_w_out[j], cos, sin)
        else:
            x = _odd_mixer(x, gain, od_w_in[j], od_pool_w[j], od_pool_scale[j], od_sgu_norm[j],
                           od_sgu_w[j], od_sgu_b[j], od_w_out[j])
        x = _ffn(x.reshape(b * s, d), layer, *ffn2, fin, layer == depth - 1).reshape(b, s, d)
    return x
```

```python
import functools

import jax
import jax.numpy as jnp
from jax import lax
from jax.experimental import pallas as pl
from jax.experimental.pallas import tpu as pltpu

F32 = jnp.float32
BF16 = jnp.bfloat16

GRID_W = 64
HEAD_DIM = 64
N_Q_HEADS = 8
N_KV_HEADS = 2
Q_PER_KV = N_Q_HEADS // N_KV_HEADS
ROPE_THETA = 10000.0
POOL_RADII = (1, 2, 4, 8)
POOL_HALO = 8
N_GROUPS = 4
GROUP = 128
EPS = 1e-6

LANES = 128
SUBLANES = 8

FFN_ROWS = 1024
FFN_SUBTILES = 4
MIX_ROWS = 512
MIX_SUBTILES = 2
VMEM_LIMIT_FFN = 56 * 1024 * 1024
VMEM_LIMIT_MIX = 48 * 1024 * 1024

Q_SCALE = HEAD_DIM ** -0.5 * 1.4426950408889634
HEAD_ORDER = tuple(g * Q_PER_KV + j for j in range(Q_PER_KV) for g in range(N_KV_HEADS))


def _rms(xf, g):
    return xf * lax.rsqrt(jnp.mean(xf * xf, axis=-1, keepdims=True) + EPS) * g


def _dot(a, b):
    return jnp.dot(a, b, preferred_element_type=F32)


def _const_spec(shape):
    zeros = (0,) * len(shape)
    return pl.BlockSpec(shape, lambda *_: zeros)


def _ffn_kernel(x_ref, g_ref, win_ref, wout_ref, fin_ref, o_ref, *, final):
    d_ff = wout_ref.shape[0]
    tm = x_ref.shape[0]
    sub = tm // FFN_SUBTILES
    for r in range(FFN_SUBTILES):
        rows = slice(sub * r, sub * (r + 1))
        x = x_ref[rows, :]
        h = _rms(x, g_ref[...]).astype(BF16)
        gu = _dot(h, win_ref[...])
        g = gu[:, :d_ff]
        u = gu[:, d_ff:]
        a = (g * jax.nn.sigmoid(g) * u).astype(BF16)
        out = x + 0.5 * _dot(a, wout_ref[...])
        if final:
            out = _rms(out, fin_ref[...])
        o_ref[rows, :] = out


def _ffn(x2, layer, gains, w_in, w_out, final_gain, final):
    n, d = x2.shape
    tm = min(FFN_ROWS, n)
    layer_spec = lambda a: pl.BlockSpec((None,) + a.shape[1:], lambda i: (layer, 0, 0))
    return pl.pallas_call(
        functools.partial(_ffn_kernel, final=final),
        grid=(n // tm,),
        in_specs=[
            pl.BlockSpec((tm, d), lambda i: (i, 0)),
            layer_spec(gains), layer_spec(w_in), layer_spec(w_out),
            _const_spec((1, d)),
        ],
        out_specs=pl.BlockSpec((tm, d), lambda i: (i, 0)),
        out_shape=jax.ShapeDtypeStruct((n, d), F32),
        compiler_params=pltpu.CompilerParams(
            dimension_semantics=("parallel",), vmem_limit_bytes=VMEM_LIMIT_FFN),
        name="ffn_final" if final else "ffn",
    )(x2, gains, w_in, w_out, final_gain)


def _even_in_kernel(x_ref, g_ref, win_ref, qg_ref, kg_ref, cos_ref, sin_ref, bd_ref,
                    gb_ref, gc_ref, qt_ref, k0_ref, k1_ref, vt0_ref, vt1_ref):
    cw = gb_ref.shape[2]
    qk0 = 3 * cw
    n_q = qt_ref.shape[1]
    sub = x_ref.shape[1] // MIX_SUBTILES
    bd = bd_ref[...]
    lane = lax.broadcasted_iota(jnp.int32, (sub, LANES), 1)
    trow = lax.broadcasted_iota(jnp.int32, (LANES, sub), 0)
    even = (lane & 1) == 0
    for t in range(MIX_SUBTILES):
        rows = slice(sub * t, sub * (t + 1))
        h = _rms(x_ref[0, rows, :], g_ref[...]).astype(BF16)
        proj = _dot(h, win_ref[...])
        gb_ref[0, rows, :] = proj[:, :cw]
        gc_ref[0, rows, :] = proj[:, cw:2 * cw] * proj[:, 2 * cw:3 * cw]
        cos = cos_ref[rows, :]
        sin = sin_ref[rows, :]
        for j in range(n_q + 1):
            c = proj[:, qk0 + LANES * j:qk0 + LANES * (j + 1)]
            sq = c * c
            hi = sq.astype(BF16)
            lo = (sq - hi.astype(F32)).astype(BF16)
            ms = _dot(jnp.concatenate([hi, lo], axis=1), bd) * (1.0 / HEAD_DIM)
            gain = qg_ref[...] if j < n_q else kg_ref[...]
            y = c * lax.rsqrt(ms + EPS) * gain
            partner = jnp.where(even, pltpu.roll(y, LANES - 1, 1), pltpu.roll(y, 1, 1))
            r = y * cos + partner * sin
            if j < n_q:
                qt_ref[0, j, :, rows] = (r * Q_SCALE).T.astype(BF16)
            else:
                k0_ref[0, rows, :] = jnp.where(lane < HEAD_DIM, r, 0.0).astype(BF16)
                k1_ref[0, rows, :] = jnp.where(lane >= HEAD_DIM, r, 0.0).astype(BF16)
        vt = proj[:, qk0 + LANES * (n_q + 1):].T
        vt0_ref[0, :, rows] = jnp.where(trow < HEAD_DIM, vt, 1.0).astype(BF16)
        vt1_ref[0, :, rows] = jnp.where(trow >= HEAD_DIM, vt, 1.0).astype(BF16)


def _even_out_kernel(x_ref, gb_ref, gc_ref, gcp_ref, gcn_ref, cw_ref, qt_ref, k0_ref, k1_ref,
                     vt0_ref, vt1_ref, wa_ref, wb_ref, o_ref):
    i = pl.program_id(1)
    last = pl.num_programs(1) - 1
    gc = gc_ref[0]
    tq = gc.shape[0]
    prev_row = jnp.where(i > 0, gcp_ref[0, SUBLANES - 1:SUBLANES, :], 0.0)
    next_row = jnp.where(i < last, gcn_ref[0, 0:1, :], 0.0)
    row = lax.broadcasted_iota(jnp.int32, gc.shape, 0)
    gc_m1 = jnp.where(row == 0, prev_row, pltpu.roll(gc, 1, 0))
    gc_p1 = jnp.where(row == tq - 1, next_row, pltpu.roll(gc, tq - 1, 0))
    cw = cw_ref[...]
    a_out = gb_ref[0] * (gc_m1 * cw[0:1, :] + gc * cw[1:2, :] + gc_p1 * cw[2:3, :])

    first = lax.broadcasted_iota(jnp.int32, (LANES, tq), 0) < HEAD_DIM
    heads = [(j, k_ref, vt_ref) for j in range(qt_ref.shape[1])
             for k_ref, vt_ref in ((k0_ref, vt0_ref), (k1_ref, vt1_ref))]
    score = lambda hd: _dot(hd[1][0], qt_ref[0, hd[0]])
    outs = []
    st = score(heads[0])
    for n, hd in enumerate(heads):
        st_next = score(heads[n + 1]) if n + 1 < len(heads) else None
        pt = jnp.exp2(st - jnp.max(st, axis=0, keepdims=True))
        outs.append(_dot(hd[2][0], pt.astype(BF16)))
        st = st_next
    pairs = []
    for j in range(qt_ref.shape[1]):
        o0, o1 = outs[2 * j], outs[2 * j + 1]
        num = jnp.where(first, o0, o1)
        den = pltpu.roll(jnp.where(first, o1, o0), HEAD_DIM, 0)
        pairs.append((num / den).T)
    b_out = jnp.concatenate(pairs, axis=1)
    o_ref[0] = (x_ref[0] + _dot(a_out.astype(BF16), wa_ref[...])
                + _dot(b_out.astype(BF16), wb_ref[...]))


def _rope_tables(seq):
    rows = seq // GRID_W
    r_idx, c_idx = jnp.meshgrid(jnp.arange(rows), jnp.arange(GRID_W), indexing='ij')
    r_idx = r_idx.reshape(-1).astype(F32)
    c_idx = c_idx.reshape(-1).astype(F32)
    n_freq = HEAD_DIM // 4
    inv = ROPE_THETA ** (-jnp.arange(n_freq, dtype=F32) / n_freq)
    ang = jnp.concatenate([r_idx[:, None] * inv, c_idx[:, None] * inv], axis=-1)
    cos = jnp.repeat(jnp.cos(ang), 2, axis=-1)
    sign = jnp.tile(jnp.array([-1.0, 1.0], F32), HEAD_DIM // 2)
    sin = jnp.repeat(jnp.sin(ang), 2, axis=-1) * sign
    reps = LANES // HEAD_DIM
    return jnp.tile(cos, (1, reps)), jnp.tile(sin, (1, reps))


def _even_mixer(x, gain, w_in, conv_w, q_g, k_g, w_out, cos, sin):
    b, s, d = x.shape
    cw = conv_w.shape[1]
    aw = N_Q_HEADS * HEAD_DIM
    kw = N_KV_HEADS * HEAD_DIM
    ts = min(MIX_ROWS, s)
    nt = s // ts

    q0 = 3 * cw
    wq = w_in[:, q0:q0 + aw].reshape(d, N_Q_HEADS, HEAD_DIM)[:, HEAD_ORDER, :].reshape(d, aw)
    w_in_p = jnp.concatenate([w_in[:, :q0], wq, w_in[:, q0 + aw:]], axis=1).astype(BF16)
    reps = LANES // HEAD_DIM
    qg = jnp.tile(q_g, reps)[None, :]
    kg = jnp.tile(k_g, reps)[None, :]
    lane = jnp.arange(LANES)
    bd = (lane[:, None] // HEAD_DIM == lane[None, :] // HEAD_DIM).astype(BF16)
    bd = jnp.concatenate([bd, bd], axis=0)

    row_spec = lambda w: pl.BlockSpec((1, ts, w), lambda bi, ti: (bi, ti, 0))
    n_pairs = aw // LANES
    qt_spec = pl.BlockSpec((1, n_pairs, LANES, ts), lambda bi, ti: (bi, 0, 0, ti))
    col_spec = pl.BlockSpec((1, kw, ts), lambda bi, ti: (bi, 0, ti))
    gb, gc, qt, k0, k1, vt0, vt1 = pl.pallas_call(
        _even_in_kernel,
        grid=(b, nt),
        in_specs=[
            row_spec(d),
            _const_spec((1, d)),
            _const_spec(w_in_p.shape),
            _const_spec((1, LANES)),
            _const_spec((1, LANES)),
            pl.BlockSpec((ts, LANES), lambda bi, ti: (ti, 0)),
            pl.BlockSpec((ts, LANES), lambda bi, ti: (ti, 0)),
            _const_spec((2 * LANES, LANES)),
        ],
        out_specs=[row_spec(cw), row_spec(cw), qt_spec, row_spec(kw), row_spec(kw),
                   col_spec, col_spec],
        out_shape=[
            jax.ShapeDtypeStruct((b, s, cw), F32),
            jax.ShapeDtypeStruct((b, s, cw), F32),
            jax.ShapeDtypeStruct((b, n_pairs, LANES, s), BF16),
            jax.ShapeDtypeStruct((b, s, kw), BF16),
            jax.ShapeDtypeStruct((b, s, kw), BF16),
            jax.ShapeDtypeStruct((b, kw, s), BF16),
            jax.ShapeDtypeStruct((b, kw, s), BF16),
        ],
        compiler_params=pltpu.CompilerParams(
            dimension_semantics=("parallel", "parallel"), vmem_limit_bytes=VMEM_LIMIT_MIX),
        name="even_in",
    )(x, gain, w_in_p, qg, kg, cos, sin, bd)

    wa = w_out[:cw].astype(BF16)
    wb = w_out[cw:].reshape(N_Q_HEADS, HEAD_DIM, d)[HEAD_ORDER, :, :].reshape(aw, d).astype(BF16)
    hb = ts // SUBLANES
    n_hb = s // SUBLANES
    seq_rows = pl.BlockSpec((1, s, kw), lambda bi, ti: (bi, 0, 0))
    seq_cols = pl.BlockSpec((1, kw, s), lambda bi, ti: (bi, 0, 0))
    return pl.pallas_call(
        _even_out_kernel,
        grid=(b, nt),
        in_specs=[
            row_spec(d), row_spec(cw), row_spec(cw),
            pl.BlockSpec((1, SUBLANES, cw), lambda bi, ti: (bi, jnp.maximum(ti * hb - 1, 0), 0)),
            pl.BlockSpec((1, SUBLANES, cw),
                         lambda bi, ti: (bi, jnp.minimum((ti + 1) * hb, n_hb - 1), 0)),
            _const_spec(conv_w.shape),
            qt_spec, seq_rows, seq_rows, seq_cols, seq_cols,
            _const_spec(wa.shape), _const_spec(wb.shape),
        ],
        out_specs=row_spec(d),
        out_shape=jax.ShapeDtypeStruct((b, s, d), F32),
        compiler_params=pltpu.CompilerParams(
            dimension_semantics=("parallel", "arbitrary"), vmem_limit_bytes=VMEM_LIMIT_MIX),
        name="even_out",
    )(x, gb, gc, gc, gc, conv_w, qt, k0, k1, vt0, vt1, wa, wb)


def _odd_in_kernel(x_ref, g_ref, win_ref, ng_ref, p_ref, u_ref, v_ref):
    pw = p_ref.shape[2]
    sw = u_ref.shape[2]
    sub = x_ref.shape[1] // MIX_SUBTILES
    for t in range(MIX_SUBTILES):
        rows = slice(sub * t, sub * (t + 1))
        h = _rms(x_ref[0, rows, :], g_ref[...]).astype(BF16)
        proj = _dot(h, win_ref[...])
        p_ref[0, rows, :] = proj[:, :pw]
        u_ref[0, rows, :] = jax.nn.gelu(proj[:, pw:pw + sw]).astype(BF16)
        vg = jax.nn.gelu(proj[:, pw + sw:])
        v_ref[0, rows, :] = _rms(vg, ng_ref[...]).astype(BF16)


def _odd_out_kernel(x_ref, p_ref, pp_ref, pn_ref, u_ref, v_ref, pbd_ref, ps_ref, ws_ref, bs_ref,
                    wc_ref, wd_ref, o_ref, ext_ref, *, seq):
    i = pl.program_id(1)
    last = pl.num_programs(1) - 1
    p = p_ref[0]
    ts = p.shape[0]
    ext_ref[0:POOL_HALO, :] = jnp.where(i > 0, pp_ref[0], 0.0)
    ext_ref[POOL_HALO:POOL_HALO + ts, :] = p
    ext_ref[POOL_HALO + ts:, :] = jnp.where(i < last, pn_ref[0], 0.0)
    t = i * ts + lax.broadcasted_iota(jnp.int32, (ts, 1), 0)
    pooled = []
    for g, r in enumerate(POOL_RADII):
        cols = slice(GROUP * g, GROUP * (g + 1))
        win = None
        for dlt in range(-r, r + 1):
            term = ext_ref[pl.ds(POOL_HALO + dlt, ts), cols]
            win = term if win is None else win + term
        cnt = (jnp.minimum(t + r, seq - 1) - jnp.maximum(t - r, 0) + 1).astype(F32)
        pooled.append(win / cnt - p[:, cols])
    pooled = jnp.concatenate(pooled, axis=1).astype(BF16)
    c_out = _dot(pooled, pbd_ref[...]) * ps_ref[...]

    u = u_ref[0].astype(F32)
    v = v_ref[0]
    rows = []
    for n in range(ts // GROUP):
        rs = slice(GROUP * n, GROUP * (n + 1))
        blocks = []
        for g in range(N_GROUPS):
            cs = slice(GROUP * g, GROUP * (g + 1))
            mixed = _dot(ws_ref[g], v[rs, cs]) + bs_ref[g]
            blocks.append(u[rs, cs] * mixed)
        rows.append(jnp.concatenate(blocks, axis=1))
    d_out = jnp.concatenate(rows, axis=0)
    o_ref[0] = (x_ref[0] + _dot(c_out.astype(BF16), wc_ref[...])
                + _dot(d_out.astype(BF16), wd_ref[...]))


def _odd_mixer(x, gain, w_in, pool_w, pool_scale, sgu_norm, sgu_w, sgu_b, w_out):
    b, s, d = x.shape
    pw = pool_scale.shape[0]
    sw = sgu_norm.shape[0]
    ts = min(MIX_ROWS, s)
    nt = s // ts
    row_spec = lambda w: pl.BlockSpec((1, ts, w), lambda bi, ti: (bi, ti, 0))

    w_in_b = w_in.astype(BF16)
    p, ug, vn = pl.pallas_call(
        _odd_in_kernel,
        grid=(b, nt),
        in_specs=[row_spec(d), _const_spec((1, d)), _const_spec(w_in_b.shape),
                  _const_spec((1, sw))],
        out_specs=[row_spec(pw), row_spec(sw), row_spec(sw)],
        out_shape=[
            jax.ShapeDtypeStruct((b, s, pw), F32),
            jax.ShapeDtypeStruct((b, s, sw), BF16),
            jax.ShapeDtypeStruct((b, s, sw), BF16),
        ],
        compiler_params=pltpu.CompilerParams(
            dimension_semantics=("parallel", "parallel"), vmem_limit_bytes=VMEM_LIMIT_MIX),
        name="odd_in",
    )(x, gain, w_in_b, sgu_norm[None, :])

    pbd = jax.scipy.linalg.block_diag(*[pool_w[g] for g in range(N_GROUPS)]).astype(BF16)
    bs = jnp.broadcast_to(sgu_b[:, :, None], sgu_b.shape + (GROUP,))
    wc = w_out[:pw].astype(BF16)
    wd = w_out[pw:].astype(BF16)
    hb = ts // POOL_HALO
    n_hb = s // POOL_HALO
    kern = functools.partial(_odd_out_kernel, seq=s)
    return pl.pallas_call(
        kern,
        grid=(b, nt),
        in_specs=[
            row_spec(d), row_spec(pw),
            pl.BlockSpec((1, POOL_HALO, pw), lambda bi, ti: (bi, jnp.maximum(ti * hb - 1, 0), 0)),
            pl.BlockSpec((1, POOL_HALO, pw),
                         lambda bi, ti: (bi, jnp.minimum((ti + 1) * hb, n_hb - 1), 0)),
            row_spec(sw), row_spec(sw),
            _const_spec(pbd.shape), _const_spec((1, pw)),
            _const_spec(sgu_w.shape), _const_spec(bs.shape),
            _const_spec(wc.shape), _const_spec(wd.shape),
        ],
        out_specs=row_spec(d),
        out_shape=jax.ShapeDtypeStruct((b, s, d), F32),
        scratch_shapes=[pltpu.VMEM((ts + 2 * POOL_HALO, pw), F32)],
        compiler_params=pltpu.CompilerParams(
            dimension_semantics=("parallel", "arbitrary"), vmem_limit_bytes=VMEM_LIMIT_MIX),
        name="odd_out",
    )(x, p, p, p, ug, vn, pbd, pool_scale[None, :], sgu_w.astype(BF16), bs, wc, wd)


def kernel(x, ffn1_norm, ffn1_w_in, ffn1_w_out, mix_norm, ffn2_norm, ffn2_w_in, ffn2_w_out,
           ev_w_in, ev_conv_w, ev_q_norm, ev_k_norm, ev_w_out,
           od_w_in, od_pool_w, od_pool_scale, od_sgu_norm, od_sgu_w, od_sgu_b, od_w_out,
           final_norm):
    b, s, d = x.shape
    depth = ffn1_norm.shape[0]
    cos, sin = _rope_tables(s)
    fin = final_norm[None, :]
    ffn1 = (ffn1_norm[:, None, :], ffn1_w_in.astype(BF16), ffn1_w_out.astype(BF16))
    ffn2 = (ffn2_norm[:, None, :], ffn2_w_in.astype(BF16), ffn2_w_out.astype(BF16))
    for layer in range(depth):
        x = _ffn(x.reshape(b * s, d), layer, *ffn1, fin, False).reshape(b, s, d)
        j = layer // 2
        gain = mix_norm[layer][None, :]
        if layer % 2 == 0:
            x = _even_mixer(x, gain, ev_w_in[j], ev_conv_w[j], ev_q_norm[j], ev_k_norm[j],
                            ev_w_out[j], cos, sin)
        else:
            x = _odd_mixer(x, gain, od_w_in[j], od_pool_w[j], od_pool_scale[j], od_sgu_norm[j],
                           od_sgu_w[j], od_sgu_b[j], od_w_out[j])
        x = _ffn(x.reshape(b * s, d), layer, *ffn2, fin, layer == depth - 1).reshape(b, s, d)
    return x
```

```python
import functools

import jax
import jax.numpy as jnp
from jax import lax
from jax.experimental import pallas as pl
from jax.experimental.pallas import tpu as pltpu

F32 = jnp.float32
BF16 = jnp.bfloat16

GRID_W = 64
HEAD_DIM = 64
N_Q_HEADS = 8
N_KV_HEADS = 2
Q_PER_KV = N_Q_HEADS // N_KV_HEADS
ROPE_THETA = 10000.0
POOL_RADII = (1, 2, 4, 8)
POOL_HALO = 8
N_GROUPS = 4
GROUP = 128
EPS = 1e-6

LANES = 128
SUBLANES = 8

FFN_ROWS = 1024
FFN_SUBTILES = 4
MIX_IN_ROWS = 2048
MIX_IN_SUBTILES = 8
ATTN_ROWS = 1024
ODD_OUT_ROWS = 1024
VMEM_LIMIT_FFN = 56 * 1024 * 1024
VMEM_LIMIT_MIX = 56 * 1024 * 1024

Q_SCALE = HEAD_DIM ** -0.5 * 1.4426950408889634


def _rms(xf, g):
    return xf * lax.rsqrt(jnp.mean(xf * xf, axis=-1, keepdims=True) + EPS) * g


def _dot(a, b):
    return jnp.dot(a, b, preferred_element_type=F32)


def _const_spec(shape):
    zeros = (0,) * len(shape)
    return pl.BlockSpec(shape, lambda *_: zeros)


def _ffn_kernel(x_ref, g_ref, win_ref, wout_ref, fin_ref, o_ref, *, final):
    d_ff = wout_ref.shape[0]
    tm = x_ref.shape[0]
    sub = tm // FFN_SUBTILES
    for r in range(FFN_SUBTILES):
        rows = slice(sub * r, sub * (r + 1))
        x = x_ref[rows, :]
        h = _rms(x, g_ref[...]).astype(BF16)
        gu = _dot(h, win_ref[...])
        g = gu[:, :d_ff]
        u = gu[:, d_ff:]
        a = (g * jax.nn.sigmoid(g) * u).astype(BF16)
        out = x + 0.5 * _dot(a, wout_ref[...])
        if final:
            out = _rms(out, fin_ref[...])
        o_ref[rows, :] = out


def _ffn(x2, layer, gains, w_in, w_out, final_gain, final):
    n, d = x2.shape
    tm = min(FFN_ROWS, n)
    layer_spec = lambda a: pl.BlockSpec((None,) + a.shape[1:], lambda i: (layer, 0, 0))
    return pl.pallas_call(
        functools.partial(_ffn_kernel, final=final),
        grid=(n // tm,),
        in_specs=[
            pl.BlockSpec((tm, d), lambda i: (i, 0)),
            layer_spec(gains), layer_spec(w_in), layer_spec(w_out),
            _const_spec((1, d)),
        ],
        out_specs=pl.BlockSpec((tm, d), lambda i: (i, 0)),
        out_shape=jax.ShapeDtypeStruct((n, d), F32),
        compiler_params=pltpu.CompilerParams(
            dimension_semantics=("parallel",), vmem_limit_bytes=VMEM_LIMIT_FFN),
        name="ffn_final" if final else "ffn",
    )(x2, gains, w_in, w_out, final_gain)


def _even_in_kernel(x_ref, g_ref, win_ref, qg_ref, kg_ref, cos_ref, sin_ref, bd_ref,
                    gb_ref, gc_ref, qt_ref, k0_ref, k1_ref, vt0_ref, vt1_ref):
    cw = gb_ref.shape[2]
    qk0 = 3 * cw
    n_q = qt_ref.shape[1]
    sub = x_ref.shape[1] // MIX_IN_SUBTILES
    bd = bd_ref[...]
    lane = lax.broadcasted_iota(jnp.int32, (sub, LANES), 1)
    trow = lax.broadcasted_iota(jnp.int32, (LANES, sub), 0)
    even = (lane & 1) == 0
    for t in range(MIX_IN_SUBTILES):
        rows = slice(sub * t, sub * (t + 1))
        h = _rms(x_ref[0, rows, :], g_ref[...]).astype(BF16)
        proj = _dot(h, win_ref[...])
        gb_ref[0, rows, :] = proj[:, :cw]
        gc_ref[0, rows, :] = proj[:, cw:2 * cw] * proj[:, 2 * cw:3 * cw]
        cos = cos_ref[rows, :]
        sin = sin_ref[rows, :]
        q_t = []
        for j in range(n_q + 1):
            c = proj[:, qk0 + LANES * j:qk0 + LANES * (j + 1)]
            sq = c * c
            hi = sq.astype(BF16)
            lo = (sq - hi.astype(F32)).astype(BF16)
            ms = _dot(jnp.concatenate([hi, lo], axis=1), bd) * (1.0 / HEAD_DIM)
            gain = qg_ref[...] if j < n_q else kg_ref[...]
            y = c * lax.rsqrt(ms + EPS) * gain
            partner = jnp.where(even, pltpu.roll(y, LANES - 1, 1), pltpu.roll(y, 1, 1))
            r = y * cos + partner * sin
            if j < n_q:
                q_t.append((r * Q_SCALE).T.astype(BF16))
            else:
                k0_ref[0, rows, :] = jnp.where(lane < HEAD_DIM, r, 0.0).astype(BF16)
                k1_ref[0, rows, :] = jnp.where(lane >= HEAD_DIM, r, 0.0).astype(BF16)
        for p in range(Q_PER_KV):
            for g in range(N_KV_HEADS):
                hd = g * Q_PER_KV + p
                src = q_t[hd // 2][HEAD_DIM * (hd % 2):HEAD_DIM * (hd % 2 + 1), :]
                qt_ref[0, p, HEAD_DIM * g:HEAD_DIM * (g + 1), rows] = src
        vt = proj[:, qk0 + LANES * (n_q + 1):].T
        vt0_ref[0, :, rows] = jnp.where(trow < HEAD_DIM, vt, 1.0).astype(BF16)
        vt1_ref[0, :, rows] = jnp.where(trow >= HEAD_DIM, vt, 1.0).astype(BF16)


def _even_out_kernel(x_ref, gb_ref, gc_ref, gcp_ref, gcn_ref, cw_ref, qt_ref, k0_ref, k1_ref,
                     vt0_ref, vt1_ref, wa_ref, wb_ref, o_ref):
    i = pl.program_id(1)
    last = pl.num_programs(1) - 1
    gc = gc_ref[0]
    tq = gc.shape[0]
    prev_row = jnp.where(i > 0, gcp_ref[0, SUBLANES - 1:SUBLANES, :], 0.0)
    next_row = jnp.where(i < last, gcn_ref[0, 0:1, :], 0.0)
    row = lax.broadcasted_iota(jnp.int32, gc.shape, 0)
    gc_m1 = jnp.where(row == 0, prev_row, pltpu.roll(gc, 1, 0))
    gc_p1 = jnp.where(row == tq - 1, next_row, pltpu.roll(gc, tq - 1, 0))
    cw = cw_ref[...]
    a_out = gb_ref[0] * (gc_m1 * cw[0:1, :] + gc * cw[1:2, :] + gc_p1 * cw[2:3, :])

    first = lax.broadcasted_iota(jnp.int32, (LANES, tq), 0) < HEAD_DIM
    heads = [(j, k_ref, vt_ref) for j in range(qt_ref.shape[1])
             for k_ref, vt_ref in ((k0_ref, vt0_ref), (k1_ref, vt1_ref))]
    score = lambda hd: _dot(hd[1][0], qt_ref[0, hd[0]])
    outs = []
    st = score(heads[0])
    for n, hd in enumerate(heads):
        st_next = score(heads[n + 1]) if n + 1 < len(heads) else None
        pt = jnp.exp2(st - jnp.max(st, axis=0, keepdims=True))
        outs.append(_dot(hd[2][0], pt.astype(BF16)))
        st = st_next
    pair_t = []
    for p in range(qt_ref.shape[1]):
        o0, o1 = outs[2 * p], outs[2 * p + 1]
        num = jnp.where(first, o0, o1)
        den = pltpu.roll(jnp.where(first, o1, o0), HEAD_DIM, 0)
        pair_t.append(num / den)
    head_t = lambda hd: pair_t[hd % Q_PER_KV][HEAD_DIM * (hd // Q_PER_KV):
                                              HEAD_DIM * (hd // Q_PER_KV + 1), :]
    b_out = jnp.concatenate(
        [jnp.concatenate([head_t(2 * c), head_t(2 * c + 1)], axis=0).T
         for c in range(N_Q_HEADS // 2)], axis=1)
    o_ref[0] = (x_ref[0] + _dot(a_out.astype(BF16), wa_ref[...])
                + _dot(b_out.astype(BF16), wb_ref[...]))


def _rope_tables(seq):
    rows = seq // GRID_W
    r_idx, c_idx = jnp.meshgrid(jnp.arange(rows), jnp.arange(GRID_W), indexing='ij')
    r_idx = r_idx.reshape(-1).astype(F32)
    c_idx = c_idx.reshape(-1).astype(F32)
    n_freq = HEAD_DIM // 4
    inv = ROPE_THETA ** (-jnp.arange(n_freq, dtype=F32) / n_freq)
    ang = jnp.concatenate([r_idx[:, None] * inv, c_idx[:, None] * inv], axis=-1)
    cos = jnp.repeat(jnp.cos(ang), 2, axis=-1)
    sign = jnp.tile(jnp.array([-1.0, 1.0], F32), HEAD_DIM // 2)
    sin = jnp.repeat(jnp.sin(ang), 2, axis=-1) * sign
    reps = LANES // HEAD_DIM
    return jnp.tile(cos, (1, reps)), jnp.tile(sin, (1, reps))


def _even_mixer(x, gain, w_in, conv_w, q_g, k_g, w_out, cos, sin):
    b, s, d = x.shape
    cw = conv_w.shape[1]
    aw = N_Q_HEADS * HEAD_DIM
    kw = N_KV_HEADS * HEAD_DIM
    ts = min(MIX_IN_ROWS, s)
    nt = s // ts

    reps = LANES // HEAD_DIM
    qg = jnp.tile(q_g, reps)[None, :]
    kg = jnp.tile(k_g, reps)[None, :]
    lane = jnp.arange(LANES)
    bd = (lane[:, None] // HEAD_DIM == lane[None, :] // HEAD_DIM).astype(BF16)
    bd = jnp.concatenate([bd, bd], axis=0)

    row_spec = lambda w: pl.BlockSpec((1, ts, w), lambda bi, ti: (bi, ti, 0))
    n_pairs = aw // LANES
    qt_spec = pl.BlockSpec((1, n_pairs, LANES, ts), lambda bi, ti: (bi, 0, 0, ti))
    col_spec = pl.BlockSpec((1, kw, ts), lambda bi, ti: (bi, 0, ti))
    gb, gc, qt, k0, k1, vt0, vt1 = pl.pallas_call(
        _even_in_kernel,
        grid=(b, nt),
        in_specs=[
            row_spec(d),
            _const_spec((1, d)),
            _const_spec(w_in.shape),
            _const_spec((1, LANES)),
            _const_spec((1, LANES)),
            pl.BlockSpec((ts, LANES), lambda bi, ti: (ti, 0)),
            pl.BlockSpec((ts, LANES), lambda bi, ti: (ti, 0)),
            _const_spec((2 * LANES, LANES)),
        ],
        out_specs=[row_spec(cw), row_spec(cw), qt_spec, row_spec(kw), row_spec(kw),
                   col_spec, col_spec],
        out_shape=[
            jax.ShapeDtypeStruct((b, s, cw), F32),
            jax.ShapeDtypeStruct((b, s, cw), F32),
            jax.ShapeDtypeStruct((b, n_pairs, LANES, s), BF16),
            jax.ShapeDtypeStruct((b, s, kw), BF16),
            jax.ShapeDtypeStruct((b, s, kw), BF16),
            jax.ShapeDtypeStruct((b, kw, s), BF16),
            jax.ShapeDtypeStruct((b, kw, s), BF16),
        ],
        compiler_params=pltpu.CompilerParams(
            dimension_semantics=("parallel", "parallel"), vmem_limit_bytes=VMEM_LIMIT_MIX),
        name="even_in",
    )(x, gain, w_in, qg, kg, cos, sin, bd)

    wa = w_out[:cw]
    wb = w_out[cw:]
    ts = min(ATTN_ROWS, s)
    nt = s // ts
    row_spec = lambda w: pl.BlockSpec((1, ts, w), lambda bi, ti: (bi, ti, 0))
    qt_spec = pl.BlockSpec((1, n_pairs, LANES, ts), lambda bi, ti: (bi, 0, 0, ti))
    hb = ts // SUBLANES
    n_hb = s // SUBLANES
    seq_rows = pl.BlockSpec((1, s, kw), lambda bi, ti: (bi, 0, 0))
    seq_cols = pl.BlockSpec((1, kw, s), lambda bi, ti: (bi, 0, 0))
    return pl.pallas_call(
        _even_out_kernel,
        grid=(b, nt),
        in_specs=[
            row_spec(d), row_spec(cw), row_spec(cw),
            pl.BlockSpec((1, SUBLANES, cw), lambda bi, ti: (bi, jnp.maximum(ti * hb - 1, 0), 0)),
            pl.BlockSpec((1, SUBLANES, cw),
                         lambda bi, ti: (bi, jnp.minimum((ti + 1) * hb, n_hb - 1), 0)),
            _const_spec(conv_w.shape),
            qt_spec, seq_rows, seq_rows, seq_cols, seq_cols,
            _const_spec(wa.shape), _const_spec(wb.shape),
        ],
        out_specs=row_spec(d),
        out_shape=jax.ShapeDtypeStruct((b, s, d), F32),
        compiler_params=pltpu.CompilerParams(
            dimension_semantics=("parallel", "arbitrary"), vmem_limit_bytes=VMEM_LIMIT_MIX),
        name="even_out",
    )(x, gb, gc, gc, gc, conv_w, qt, k0, k1, vt0, vt1, wa, wb)


def _odd_in_kernel(x_ref, g_ref, win_ref, ng_ref, p_ref, u_ref, v_ref):
    pw = p_ref.shape[2]
    sw = u_ref.shape[2]
    sub = x_ref.shape[1] // MIX_IN_SUBTILES
    for t in range(MIX_IN_SUBTILES):
        rows = slice(sub * t, sub * (t + 1))
        h = _rms(x_ref[0, rows, :], g_ref[...]).astype(BF16)
        proj = _dot(h, win_ref[...])
        p_ref[0, rows, :] = proj[:, :pw]
        u_ref[0, rows, :] = jax.nn.gelu(proj[:, pw:pw + sw]).astype(BF16)
        vg = jax.nn.gelu(proj[:, pw + sw:])
        v_ref[0, rows, :] = _rms(vg, ng_ref[...]).astype(BF16)


def _odd_out_kernel(x_ref, p_ref, pp_ref, pn_ref, u_ref, v_ref, pbd_ref, ps_ref, ws_ref, bs_ref,
                    wc_ref, wd_ref, o_ref, ext_ref, *, seq):
    i = pl.program_id(1)
    last = pl.num_programs(1) - 1
    p = p_ref[0]
    ts = p.shape[0]
    ext_ref[0:POOL_HALO, :] = jnp.where(i > 0, pp_ref[0], 0.0)
    ext_ref[POOL_HALO:POOL_HALO + ts, :] = p
    ext_ref[POOL_HALO + ts:, :] = jnp.where(i < last, pn_ref[0], 0.0)
    t = i * ts + lax.broadcasted_iota(jnp.int32, (ts, 1), 0)
    pooled = []
    for g, r in enumerate(POOL_RADII):
        cols = slice(GROUP * g, GROUP * (g + 1))
        win = None
        for dlt in range(-r, r + 1):
            term = ext_ref[pl.ds(POOL_HALO + dlt, ts), cols]
            win = term if win is None else win + term
        cnt = (jnp.minimum(t + r, seq - 1) - jnp.maximum(t - r, 0) + 1).astype(F32)
        pooled.append(win / cnt - p[:, cols])
    pooled = jnp.concatenate(pooled, axis=1).astype(BF16)
    c_out = _dot(pooled, pbd_ref[...]) * ps_ref[...]

    u = u_ref[0].astype(F32)
    v = v_ref[0]
    rows = []
    for n in range(ts // GROUP):
        rs = slice(GROUP * n, GROUP * (n + 1))
        blocks = []
        for g in range(N_GROUPS):
            cs = slice(GROUP * g, GROUP * (g + 1))
            mixed = _dot(ws_ref[g], v[rs, cs]) + bs_ref[g]
            blocks.append(u[rs, cs] * mixed)
        rows.append(jnp.concatenate(blocks, axis=1))
    d_out = jnp.concatenate(rows, axis=0)
    o_ref[0] = (x_ref[0] + _dot(c_out.astype(BF16), wc_ref[...])
                + _dot(d_out.astype(BF16), wd_ref[...]))


def _odd_mixer(x, gain, w_in, pool_w, pool_scale, sgu_norm, sgu_w, sgu_b, w_out):
    b, s, d = x.shape
    pw = pool_scale.shape[0]
    sw = sgu_norm.shape[0]
    ts = min(MIX_IN_ROWS, s)
    nt = s // ts
    row_spec = lambda w: pl.BlockSpec((1, ts, w), lambda bi, ti: (bi, ti, 0))

    p, ug, vn = pl.pallas_call(
        _odd_in_kernel,
        grid=(b, nt),
        in_specs=[row_spec(d), _const_spec((1, d)), _const_spec(w_in.shape),
                  _const_spec((1, sw))],
        out_specs=[row_spec(pw), row_spec(sw), row_spec(sw)],
        out_shape=[
            jax.ShapeDtypeStruct((b, s, pw), F32),
            jax.ShapeDtypeStruct((b, s, sw), BF16),
            jax.ShapeDtypeStruct((b, s, sw), BF16),
        ],
        compiler_params=pltpu.CompilerParams(
            dimension_semantics=("parallel", "parallel"), vmem_limit_bytes=VMEM_LIMIT_MIX),
        name="odd_in",
    )(x, gain, w_in, sgu_norm[None, :])

    pbd = jax.scipy.linalg.block_diag(*[pool_w[g] for g in range(N_GROUPS)]).astype(BF16)
    bs = jnp.broadcast_to(sgu_b[:, :, None], sgu_b.shape + (GROUP,))
    wc = w_out[:pw]
    wd = w_out[pw:]
    ts = min(ODD_OUT_ROWS, s)
    nt = s // ts
    row_spec = lambda w: pl.BlockSpec((1, ts, w), lambda bi, ti: (bi, ti, 0))
    hb = ts // POOL_HALO
    n_hb = s // POOL_HALO
    kern = functools.partial(_odd_out_kernel, seq=s)
    return pl.pallas_call(
        kern,
        grid=(b, nt),
        in_specs=[
            row_spec(d), row_spec(pw),
            pl.BlockSpec((1, POOL_HALO, pw), lambda bi, ti: (bi, jnp.maximum(ti * hb - 1, 0), 0)),
            pl.BlockSpec((1, POOL_HALO, pw),
                         lambda bi, ti: (bi, jnp.minimum((ti + 1) * hb, n_hb - 1), 0)),
            row_spec(sw), row_spec(sw),
            _const_spec(pbd.shape), _const_spec((1, pw)),
            _const_spec(sgu_w.shape), _const_spec(bs.shape),
            _const_spec(wc.shape), _const_spec(wd.shape),
        ],
        out_specs=row_spec(d),
        out_shape=jax.ShapeDtypeStruct((b, s, d), F32),
        scratch_shapes=[pltpu.VMEM((ts + 2 * POOL_HALO, pw), F32)],
        compiler_params=pltpu.CompilerParams(
            dimension_semantics=("parallel", "arbitrary"), vmem_limit_bytes=VMEM_LIMIT_MIX),
        name="odd_out",
    )(x, p, p, p, ug, vn, pbd, pool_scale[None, :], sgu_w.astype(BF16), bs, wc, wd)


def kernel(x, ffn1_norm, ffn1_w_in, ffn1_w_out, mix_norm, ffn2_norm, ffn2_w_in, ffn2_w_out,
           ev_w_in, ev_conv_w, ev_q_norm, ev_k_norm, ev_w_out,
           od_w_in, od_pool_w, od_pool_scale, od_sgu_norm, od_sgu_w, od_sgu_b, od_w_out,
           final_norm):
    b, s, d = x.shape
    depth = ffn1_norm.shape[0]
    cos, sin = _rope_tables(s)
    fin = final_norm[None, :]
    ffn1 = (ffn1_norm[:, None, :], ffn1_w_in.astype(BF16), ffn1_w_out.astype(BF16))
    ffn2 = (ffn2_norm[:, None, :], ffn2_w_in.astype(BF16), ffn2_w_out.astype(BF16))
    ev_w_in, ev_w_out = ev_w_in.astype(BF16), ev_w_out.astype(BF16)
    od_w_in, od_w_out = od_w_in.astype(BF16), od_w_out.astype(BF16)
    for layer in range(depth):
        x = _ffn(x.reshape(b * s, d), layer, *ffn1, fin, False).reshape(b, s, d)
        j = layer // 2
        gain = mix_norm[layer][None, :]
        if layer % 2 == 0:
            x = _even_mixer(x, gain, ev_w_in[j], ev_conv_w[j], ev_q_norm[j], ev_k_norm[j],
                            ev_w_out[j], cos, sin)
        else:
            x = _odd_mixer(x, gain, od_w_in[j], od_pool_w[j], od_pool_scale[j], od_sgu_norm[j],
                           od_sgu_w[j], od_sgu_b[j], od_w_out[j])
        x = _ffn(x.reshape(b * s, d), layer, *ffn2, fin, layer == depth - 1).reshape(b, s, d)
    return x
```
